```python
import jax, jax.numpy as jnp
from jax import lax
import numpy as np

D_MODEL = 1024
BATCH = 8
SEQ = 2048
DEPTH = 4
DEC_BATCH = 128
DEC_SEQ = 8
PAST_LEN = 16384
PAGE_SIZE = 128

D_A = D_MODEL // 2
D_B = D_MODEL // 2
D_C = D_MODEL // 2
D_D = D_MODEL // 2
CONV_A = 31
CHUNK = 128
G_B = 4
CONV_C = 3
POOL_WINDOWS = (2, 4, 8, 16)
G_D = len(POOL_WINDOWS)
DG_D = D_D // G_D
POOL_BUF = max(POOL_WINDOWS) - 1
N_EXPERTS = 32
TOP_K = 4
D_FF = D_MODEL
SWIGLU_LIMIT = 7.0
SWIGLU_ALPHA = 1.702
MOE_BLOCK = 128
N_EVEN = (DEPTH + 1) // 2
N_ODD = DEPTH // 2
DN_ALPHA = (2 * DEPTH) ** 0.25
DN_BETA = (8 * DEPTH) ** -0.25
LN_EPS = 1e-5

kernel_name = 'hybrid_conv_gmlp_pool_moe_decoder_step'


def layer_norm(x, g, b):
    xf = x.astype(jnp.float32)
    mu = jnp.mean(xf, axis=-1, keepdims=True)
    var = jnp.mean(jnp.square(xf - mu), axis=-1, keepdims=True)
    return ((xf - mu) * lax.rsqrt(var + LN_EPS) * g + b).astype(x.dtype)


def prepend(hist, new, n_rows):
    if hist is None:
        hist = jnp.zeros((new.shape[0], n_rows, new.shape[-1]), new.dtype)
    return jnp.concatenate([hist, new], axis=1)


def causal_dwconv(ext, w):
    return lax.conv_general_dilated(ext, w[:, None, :].astype(ext.dtype), window_strides=(1,),
                                    padding='VALID', dimension_numbers=('NWC', 'WIO', 'NWC'),
                                    feature_group_count=ext.shape[-1])


def spatial_gate(v, w_s, b_s):
    Bn, T, C = v.shape
    L = min(T, CHUNK)
    vr = v.reshape(Bn, T // L, L, G_B, C // G_B)
    w = w_s[:, :L, :L] * jnp.tril(jnp.ones((L, L), w_s.dtype))
    s = jnp.einsum('gij,bnjgc->bnigc', w, vr) + b_s[:, :L].T[None, None, :, :, None]
    return s.reshape(Bn, T, C)


def multi_scale_pool(ext, pos0):
    Bn, L, C = ext.shape
    T = L - POOL_BUF
    xf = ext.astype(jnp.float32)
    cs = jnp.concatenate([jnp.zeros((Bn, 1, C), jnp.float32), jnp.cumsum(xf, axis=1)], axis=1)
    end = cs[:, POOL_BUF + 1:]
    pos = pos0 + jnp.arange(T)
    outs = []
    for g, w in enumerate(POOL_WINDOWS):
        sl = slice(g * DG_D, (g + 1) * DG_D)
        start = cs[:, POOL_BUF + 1 - w:POOL_BUF + 1 - w + T, sl]
        cnt = jnp.minimum(w, pos + 1).astype(jnp.float32)[None, :, None]
        outs.append((end[..., sl] - start) / cnt)
    mean = jnp.concatenate(outs, axis=-1)
    return (mean - xf[:, POOL_BUF:]).astype(ext.dtype)


def even_mixer(h, hist_a, w_in, w_dw, b_dw, ln_a_g, ln_a_b, ln_v_g, ln_v_b, w_s, b_s, w_out):
    z = h @ w_in
    a_val, a_gate, zb = z[..., :D_A], z[..., D_A:2 * D_A], z[..., 2 * D_A:]
    ext_a = prepend(hist_a, a_val * jax.nn.sigmoid(a_gate), CONV_A - 1)
    conv = causal_dwconv(ext_a, w_dw) + b_dw
    y_a = jax.nn.silu(layer_norm(conv, ln_a_g, ln_a_b))
    zb = jax.nn.gelu(zb, approximate=False)
    u, v = zb[..., :D_B], zb[..., D_B:]
    v = layer_norm(v, ln_v_g, ln_v_b)
    y_b = u * spatial_gate(v, w_s, b_s)
    y = jnp.concatenate([y_a, y_b], axis=-1) @ w_out
    return y, ext_a[:, -(CONV_A - 1):], v


def odd_mixer(h, hist_c, hist_d, pos0, w_in, w_conv, w_pool, scale_pool, w_out):
    z = h @ w_in
    b_g, c_g, xt, p = (z[..., :D_C], z[..., D_C:2 * D_C], z[..., 2 * D_C:3 * D_C], z[..., 3 * D_C:])
    ext_c = prepend(hist_c, c_g * xt, CONV_C - 1)
    y_c = b_g * causal_dwconv(ext_c, w_conv)
    ext_d = prepend(hist_d, p, POOL_BUF)
    pooled = multi_scale_pool(ext_d, pos0)
    Bn, T, _ = pooled.shape
    y_d = jnp.einsum('btgc,gcd->btgd', pooled.reshape(Bn, T, G_D, DG_D), w_pool).reshape(Bn, T, D_D) * scale_pool
    y = jnp.concatenate([y_c, y_d], axis=-1) @ w_out
    return y, ext_c[:, -(CONV_C - 1):], ext_d[:, -POOL_BUF:]


def moe(x, w_router, b_router, w1, b1, w2, b2):
    Bn, T, D = x.shape
    xt = x.reshape(-1, D)
    N = xt.shape[0]
    logits = xt.astype(jnp.float32) @ w_router.astype(jnp.float32) + b_router.astype(jnp.float32)
    top_v, top_i = lax.top_k(logits, TOP_K)
    gates = jax.nn.softmax(top_v, axis=-1).astype(x.dtype)
    A = N * TOP_K
    flat_e = top_i.reshape(-1)
    order = jnp.argsort(flat_e)
    sorted_e = flat_e[order]
    tok = order // TOP_K
    counts = jnp.bincount(flat_e, length=N_EXPERTS)
    padded = (counts + MOE_BLOCK - 1) // MOE_BLOCK * MOE_BLOCK
    pad_end = jnp.cumsum(padded)
    pad_start = pad_end - padded
    grp_start = jnp.cumsum(counts) - counts
    dest = pad_start[sorted_e] + jnp.arange(A) - grp_start[sorted_e]
    n_blocks = -(-(A + N_EXPERTS * (MOE_BLOCK - 1)) // MOE_BLOCK)
    xs = jnp.zeros((n_blocks * MOE_BLOCK, D), x.dtype).at[dest].set(xt[tok])
    block_e = jnp.minimum(jnp.searchsorted(pad_end, jnp.arange(n_blocks) * MOE_BLOCK, side='right'), N_EXPERTS - 1)

    def expert_block(args):
        xb, e = args
        hdn = xb @ w1[e] + b1[e]
        glu, lin = hdn[..., :D_FF], hdn[..., D_FF:]
        glu = jnp.minimum(glu, SWIGLU_LIMIT)
        lin = jnp.clip(lin, -SWIGLU_LIMIT, SWIGLU_LIMIT)
        act = glu * jax.nn.sigmoid(SWIGLU_ALPHA * glu) * (lin + 1)
        return act @ w2[e] + b2[e]

    ys = lax.map(expert_block, (xs.reshape(n_blocks, MOE_BLOCK, D), block_e)).reshape(-1, D)
    contrib = ys[dest] * gates.reshape(-1)[order][:, None]
    y = jnp.zeros_like(xt).at[tok].add(contrib)
    return y.reshape(Bn, T, D)


def run_group(x, c, pos0, hist_a, hist_c, hist_d,
              w_ada, b_ada, ln_g, ln_b,
              w_in_even, w_dw_a, b_dw_a, ln_a_g, ln_a_b, ln_v_g, ln_v_b, w_spatial, b_spatial, w_out_even,
              w_in_odd, w_conv_c, w_pool_d, scale_pool_d, w_out_odd,
              w_router, b_router, w_exp_in, b_exp_in, w_exp_out, b_exp_out):
    Bn = x.shape[0]
    c_act = jax.nn.silu(c)
    new_a, new_v, new_c, new_d = [], [], [], []
    for l in range(DEPTH):
        mod = (c_act @ w_ada[l] + b_ada[l]).reshape(Bn, 6, 1, D_MODEL)
        sh1, sc1, gt1, sh2, sc2, gt2 = (mod[:, k] for k in range(6))
        h = x * (1 + sc1) + sh1
        i = l // 2
        if l % 2 == 0:
            ha = None if hist_a is None else hist_a[i]
            y, na, v = even_mixer(h, ha, w_in_even[i], w_dw_a[i], b_dw_a[i], ln_a_g[i], ln_a_b[i],
                                  ln_v_g[i], ln_v_b[i], w_spatial[i], b_spatial[i], w_out_even[i])
            new_a.append(na)
            new_v.append(v)
        else:
            hc = None if hist_c is None else hist_c[i]
            hd = None if hist_d is None else hist_d[i]
            y, nc, nd = odd_mixer(h, hc, hd, pos0, w_in_odd[i], w_conv_c[i], w_pool_d[i],
                                  scale_pool_d[i], w_out_odd[i])
            new_c.append(nc)
            new_d.append(nd)
        x = layer_norm(DN_ALPHA * x + gt1 * y, ln_g[l, 0], ln_b[l, 0])
        h = x * (1 + sc2) + sh2
        f = moe(h, w_router[l], b_router[l], w_exp_in[l], b_exp_in[l], w_exp_out[l], b_exp_out[l])
        x = layer_norm(DN_ALPHA * x + gt2 * f, ln_g[l, 1], ln_b[l, 1])
    return x, jnp.stack(new_a), jnp.stack(new_v), jnp.stack(new_c), jnp.stack(new_d)


def setup_inputs(seed: int = 0) -> dict:
    key = jax.random.key(seed)
    ks = iter(jax.random.split(key, 40))

    def nrm(shape, scale):
        return jax.random.normal(next(ks), shape, jnp.float32) * scale

    gate_slots = jnp.zeros((6, D_MODEL), jnp.float32).at[jnp.array([2, 5])].set(1.0).reshape(-1)
    return {
        'x_prompt': nrm((BATCH, SEQ, D_MODEL), 1.0),
        'x_sample': nrm((DEC_BATCH, DEC_SEQ, D_MODEL), 1.0),
        'state_conv_a': nrm((N_EVEN, DEC_BATCH, CONV_A - 1, D_A), 0.5),
        'state_conv_c': nrm((N_ODD, DEC_BATCH, CONV_C - 1, D_C), 1.0),
        'state_pool_d': nrm((N_ODD, DEC_BATCH, POOL_BUF, D_D), 1.0),
        'c_prompt': nrm((BATCH, D_MODEL), 1.0),
        'c_sample': nrm((DEC_BATCH, D_MODEL), 1.0),
        'w_ada': nrm((DEPTH, D_MODEL, 6 * D_MODEL), 0.1 * D_MODEL ** -0.5),
        'b_ada': gate_slots[None] + nrm((DEPTH, 6 * D_MODEL), 0.02),
        'ln_g': 1.0 + nrm((DEPTH, 2, D_MODEL), 0.02),
        'ln_b': nrm((DEPTH, 2, D_MODEL), 0.02),
        'w_in_even': nrm((N_EVEN, D_MODEL, 2 * D_A + 2 * D_B), D_MODEL ** -0.5),
        'w_dw_a': nrm((N_EVEN, CONV_A, D_A), CONV_A ** -0.5),
        'b_dw_a': nrm((N_EVEN, D_A), 0.02),
        'ln_a_g': 1.0 + nrm((N_EVEN, D_A), 0.02),
        'ln_a_b': nrm((N_EVEN, D_A), 0.02),
        'ln_v_g': 1.0 + nrm((N_EVEN, D_B), 0.02),
        'ln_v_b': nrm((N_EVEN, D_B), 0.02),
        'w_spatial': nrm((N_EVEN, G_B, CHUNK, CHUNK), CHUNK ** -0.5),
        'b_spatial': 1.0 + nrm((N_EVEN, G_B, CHUNK), 0.02),
        'w_out_even': nrm((N_EVEN, D_A + D_B, D_MODEL), DN_BETA * (D_A + D_B) ** -0.5),
        'w_in_odd': nrm((N_ODD, D_MODEL, 3 * D_C + D_D), D_MODEL ** -0.5),
        'w_conv_c': nrm((N_ODD, CONV_C, D_C), CONV_C ** -0.5),
        'w_pool_d': nrm((N_ODD, G_D, DG_D, DG_D), DG_D ** -0.5),
        'scale_pool_d': 1.0 + nrm((N_ODD, D_D), 0.02),
        'w_out_odd': nrm((N_ODD, D_C + D_D, D_MODEL), DN_BETA * (D_C + D_D) ** -0.5),
        'w_router': nrm((DEPTH, D_MODEL, N_EXPERTS), D_MODEL ** -0.5),
        'b_router': nrm((DEPTH, N_EXPERTS), 0.01),
        'w_exp_in': nrm((DEPTH, N_EXPERTS, D_MODEL, 2 * D_FF), D_MODEL ** -0.5),
        'b_exp_in': nrm((DEPTH, N_EXPERTS, 2 * D_FF), 0.02),
        'w_exp_out': nrm((DEPTH, N_EXPERTS, D_FF, D_MODEL), DN_BETA * D_FF ** -0.5),
        'b_exp_out': nrm((DEPTH, N_EXPERTS, D_MODEL), 0.02),
    }


def reference(x_prompt, x_sample, state_conv_a, state_conv_c, state_pool_d, c_prompt, c_sample,
              w_ada, b_ada, ln_g, ln_b,
              w_in_even, w_dw_a, b_dw_a, ln_a_g, ln_a_b, ln_v_g, ln_v_b, w_spatial, b_spatial, w_out_even,
              w_in_odd, w_conv_c, w_pool_d, scale_pool_d, w_out_odd,
              w_router, b_router, w_exp_in, b_exp_in, w_exp_out, b_exp_out):
    weights = (w_ada, b_ada, ln_g, ln_b,
               w_in_even, w_dw_a, b_dw_a, ln_a_g, ln_a_b, ln_v_g, ln_v_b, w_spatial, b_spatial, w_out_even,
               w_in_odd, w_conv_c, w_pool_d, scale_pool_d, w_out_odd,
               w_router, b_router, w_exp_in, b_exp_in, w_exp_out, b_exp_out)
    y_prompt, new_conv_a_prompt, _, new_conv_c_prompt, new_pool_d_prompt = run_group(
        x_prompt, c_prompt, 0, None, None, None, *weights)
    y_sample, new_conv_a_sample, new_chunk_v_sample, new_conv_c_sample, new_pool_d_sample = run_group(
        x_sample, c_sample, PAST_LEN, state_conv_a, state_conv_c, state_pool_d, *weights)
    return (y_prompt, y_sample, new_conv_a_prompt, new_conv_a_sample, new_chunk_v_sample,
            new_conv_c_prompt, new_conv_c_sample, new_pool_d_prompt, new_pool_d_sample)
```

```python
import functools

import jax
import jax.numpy as jnp
from jax import lax
from jax.experimental import pallas as pl
from jax.experimental.pallas import tpu as pltpu

F32 = jnp.float32
BF16 = jnp.bfloat16

D_MODEL = 1024
BATCH = 8
SEQ = 2048
DEPTH = 4
DEC_BATCH = 128
DEC_SEQ = 8
PAST_LEN = 16384
HALF = 512
CONV_A = 31
CHUNK = 128
G_B = 4
CONV_C = 3
POOL_WINDOWS = (2, 4, 8, 16)
POOL_BUF = 15
N_EXPERTS = 32
TOP_K = 4
D_FF = D_MODEL
SWIGLU_LIMIT = 7.0
SWIGLU_ALPHA = 1.702
DN_ALPHA = (2 * DEPTH) ** 0.25
LN_EPS = 1e-5

N_PROMPT = BATCH * SEQ
N_SAMPLE = DEC_BATCH * DEC_SEQ
N_TOK = N_PROMPT + N_SAMPLE

TT = 256
NT = SEQ // TT
SB = 32
NSB = DEC_BATCH // SB
SROWS = SB * DEC_SEQ
HALO_A = 32
HALO_C = 8
HALO_D = 16
CONV_ROWS = 32

TM_R = 512
TM_X = 256
N_ASSIGN = N_TOK * TOP_K
N_XTILES = -(-(N_ASSIGN + N_EXPERTS * (TM_X - 1)) // TM_X)
N_XROWS = N_XTILES * TM_X

VMEM_LIMIT = 56 * 1024 * 1024


def _cparams(n_axes):
    return pltpu.CompilerParams(dimension_semantics=("arbitrary",) * n_axes, vmem_limit_bytes=VMEM_LIMIT)


def _layer_norm(x, g, b):
    mu = jnp.mean(x, axis=-1, keepdims=True)
    xc = x - mu
    var = jnp.mean(xc * xc, axis=-1, keepdims=True)
    return xc * lax.rsqrt(var + LN_EPS) * g + b


def _sigmoid(x):
    return jax.nn.sigmoid(x)


def _silu(x):
    return x * _sigmoid(x)


def _gelu(x):
    return 0.5 * x * (1.0 + lax.erf(x * (0.5 ** 0.5)))


def _bdot(a, b):
    return jnp.dot(a.astype(BF16), b.astype(BF16), preferred_element_type=F32)


def _mod_kernel(c_ref, w_ref, b_ref, o_ref):
    c = c_ref[...]
    o_ref[0, 0] = _bdot(_silu(c), w_ref[0]) + b_ref[0]


def _modulation(c_all, w_ada, b_ada):
    nb = c_all.shape[0]
    return pl.pallas_call(
        _mod_kernel,
        grid=(DEPTH, 6),
        in_specs=[
            pl.BlockSpec((nb, D_MODEL), lambda l, k: (0, 0)),
            pl.BlockSpec((1, D_MODEL, D_MODEL), lambda l, k: (l, 0, k)),
            pl.BlockSpec((1, 1, D_MODEL), lambda l, k: (l, 0, k)),
        ],
        out_specs=pl.BlockSpec((1, 1, nb, D_MODEL), lambda l, k: (l, k, 0, 0)),
        out_shape=jax.ShapeDtypeStruct((DEPTH, 6, nb, D_MODEL), F32),
        compiler_params=_cparams(2),
        name="adaln_modulation",
    )(c_all, w_ada, b_ada.reshape(DEPTH, 1, 6 * D_MODEL))


def _mixer_tail(x, y_lo, y_hi, w_out_ref, gt1, lng, lnb, sc2, sh2):
    y = _bdot(y_lo, w_out_ref[0:HALF, :]) + _bdot(y_hi, w_out_ref[HALF:2 * HALF, :])
    x1 = _layer_norm(DN_ALPHA * x + gt1 * y, lng, lnb)
    h2 = x1 * (1.0 + sc2) + sh2
    return x1, h2


def _carry_halo(ext_ref, halo, t):
    @pl.when(t == 0)
    def _():
        ext_ref[0:halo, :] = jnp.zeros((halo, HALF), F32)

    @pl.when(t > 0)
    def _():
        ext_ref[0:halo, :] = ext_ref[TT:TT + halo, :]


def _prompt_kernel(parity, has_pre, *refs):
    it = iter(refs)
    if has_pre:
        x1p_ref, f_ref, modp_ref, lpg_ref, lpb_ref = (next(it) for _ in range(5))
    else:
        x_ref = next(it)
    mod_ref, w_in_ref = next(it), next(it)
    if parity == 0:
        wdw_ref, bdw_ref, lag_ref, lab_ref, lvg_ref, lvb_ref, wsp_ref, bsp_ref = (next(it) for _ in range(8))
    else:
        wc_ref, wpool_ref, spool_ref = (next(it) for _ in range(3))
    w_out_ref, lng_ref, lnb_ref = (next(it) for _ in range(3))
    x1_ref, h2f_ref, h2b_ref = (next(it) for _ in range(3))
    if parity == 0:
        conva_ref, ext_ref = next(it), next(it)
    else:
        convc_ref, poold_ref, extc_ref, extd_ref = (next(it) for _ in range(4))

    t = pl.program_id(1)
    if has_pre:
        x = _layer_norm(DN_ALPHA * x1p_ref[...] + modp_ref[5, 0] * f_ref[...], lpg_ref[...], lpb_ref[...])
    else:
        x = x_ref[...]
    sh1, sc1, gt1, sh2, sc2 = (mod_ref[k, 0] for k in range(5))
    h = x * (1.0 + sc1) + sh1
    z = _bdot(h, w_in_ref[...])

    if parity == 0:
        a_val, a_gate = z[:, 0:HALF], z[:, HALF:2 * HALF]
        zu, zv = z[:, 2 * HALF:3 * HALF], z[:, 3 * HALF:4 * HALF]
        _carry_halo(ext_ref, HALO_A, t)
        ext_ref[HALO_A:HALO_A + TT, :] = a_val * _sigmoid(a_gate)
        bias = bdw_ref[...]
        parts = []
        for r in range(TT // CONV_ROWS):
            acc = jnp.zeros((CONV_ROWS, HALF), F32) + bias
            for k in range(CONV_A):
                off = r * CONV_ROWS + k + (HALO_A - CONV_A + 1)
                acc = acc + wdw_ref[k:k + 1, :] * ext_ref[off:off + CONV_ROWS, :]
            parts.append(acc)
        conv = jnp.concatenate(parts, axis=0)
        y_lo = _silu(_layer_norm(conv, lag_ref[...], lab_ref[...]))

        @pl.when(t == NT - 1)
        def _():
            conva_ref[0] = ext_ref[TT + HALO_A - (CONV_A - 1):TT + HALO_A, :]

        u = _gelu(zu)
        v = _layer_norm(_gelu(zv), lvg_ref[...], lvb_ref[...])
        row = lax.broadcasted_iota(jnp.int32, (CHUNK, CHUNK), 0)
        col = lax.broadcasted_iota(jnp.int32, (CHUNK, CHUNK), 1)
        chunks = []
        for c in range(TT // CHUNK):
            groups = []
            for g in range(G_B):
                wg = jnp.where(row >= col, wsp_ref[g], 0.0)
                vg = v[c * CHUNK:(c + 1) * CHUNK, g * CHUNK:(g + 1) * CHUNK]
                groups.append(_bdot(wg, vg) + bsp_ref[g])
            chunks.append(jnp.concatenate(groups, axis=1))
        y_hi = u * jnp.concatenate(chunks, axis=0)
    else:
        b_g, c_g = z[:, 0:HALF], z[:, HALF:2 * HALF]
        xt, p = z[:, 2 * HALF:3 * HALF], z[:, 3 * HALF:4 * HALF]
        _carry_halo(extc_ref, HALO_C, t)
        extc_ref[HALO_C:HALO_C + TT, :] = c_g * xt
        conv = jnp.zeros((TT, HALF), F32)
        for k in range(CONV_C):
            off = k + (HALO_C - CONV_C + 1)
            conv = conv + wc_ref[k:k + 1, :] * extc_ref[off:off + TT, :]
        y_lo = b_g * conv
        _carry_halo(extd_ref, HALO_D, t)
        extd_ref[HALO_D:HALO_D + TT, :] = p
        pos = t * TT + lax.broadcasted_iota(jnp.int32, (TT, CHUNK), 0)
        groups = []
        for g, w in enumerate(POOL_WINDOWS):
            lanes = slice(g * CHUNK, (g + 1) * CHUNK)
            acc = extd_ref[HALO_D:HALO_D + TT, lanes]
            for j in range(1, w):
                acc = acc + extd_ref[HALO_D - j:HALO_D - j + TT, lanes]
            cnt = jnp.minimum(w, pos + 1).astype(F32)
            pooled = acc / cnt - p[:, lanes]
            groups.append(_bdot(pooled, wpool_ref[g]))
        y_hi = jnp.concatenate(groups, axis=1) * spool_ref[...]

        @pl.when(t == NT - 1)
        def _():
            convc_ref[0] = extc_ref[TT + HALO_C - (CONV_C - 1):TT + HALO_C, :]
            poold_ref[0] = extd_ref[TT + HALO_D - POOL_BUF:TT + HALO_D, :]

    x1, h2 = _mixer_tail(x, y_lo, y_hi, w_out_ref, gt1, lng_ref[...], lnb_ref[...], sc2, sh2)
    x1_ref[...] = x1
    h2f_ref[...] = h2
    h2b_ref[...] = h2.astype(BF16)


def _full(shape):
    n = len(shape)
    return pl.BlockSpec(shape, lambda b, t: (0,) * n)


def _prompt_layer(parity, x_or_pre, mod_p, w_in, mix_w, w_out, lng, lnb):
    has_pre = isinstance(x_or_pre, tuple)
    rows = pl.BlockSpec((TT, D_MODEL), lambda b, t: (b * NT + t, 0))
    mod_spec = pl.BlockSpec((6, 1, 1, D_MODEL), lambda b, t: (0, b, 0, 0))
    vec_d = _full((1, D_MODEL))
    vec_h = _full((1, HALF))
    if has_pre:
        x1p, f_all, modp, lpg, lpb = x_or_pre
        args = [x1p, f_all, modp, lpg, lpb]
        specs = [rows, rows, mod_spec, vec_d, vec_d]
    else:
        args = [x_or_pre]
        specs = [rows]
    args += [mod_p, w_in]
    specs += [mod_spec, _full((D_MODEL, 4 * HALF))]
    if parity == 0:
        wdw, bdw, lag, lab, lvg, lvb, wsp, bsp = mix_w
        args += [wdw, bdw, lag, lab, lvg, lvb, wsp, bsp]
        specs += [_full((CONV_A, HALF)), vec_h, vec_h, vec_h, vec_h, vec_h,
                  _full((G_B, CHUNK, CHUNK)), _full((G_B, CHUNK, 1))]
        state_shapes = [jax.ShapeDtypeStruct((BATCH, CONV_A - 1, HALF), F32)]
        state_specs = [pl.BlockSpec((1, CONV_A - 1, HALF), lambda b, t: (b, 0, 0))]
        scratch = [pltpu.VMEM((HALO_A + TT, HALF), F32)]
    else:
        wc, wpool, spool = mix_w
        args += [wc, wpool, spool]
        specs += [_full((CONV_C, HALF)), _full((G_B, CHUNK, CHUNK)), vec_h]
        state_shapes = [jax.ShapeDtypeStruct((BATCH, CONV_C - 1, HALF), F32),
                        jax.ShapeDtypeStruct((BATCH, POOL_BUF, HALF), F32)]
        state_specs = [pl.BlockSpec((1, CONV_C - 1, HALF), lambda b, t: (b, 0, 0)),
                       pl.BlockSpec((1, POOL_BUF, HALF), lambda b, t: (b, 0, 0))]
        scratch = [pltpu.VMEM((HALO_C + TT, HALF), F32), pltpu.VMEM((HALO_D + TT, HALF), F32)]
    args += [w_out, lng, lnb]
    specs += [_full((2 * HALF, D_MODEL)), vec_d, vec_d]
    out_shapes = [jax.ShapeDtypeStruct((N_PROMPT, D_MODEL), F32),
                  jax.ShapeDtypeStruct((N_PROMPT, D_MODEL), F32),
                  jax.ShapeDtypeStruct((N_PROMPT, D_MODEL), BF16)] + state_shapes
    out_specs = [rows, rows, rows] + state_specs
    return pl.pallas_call(
        functools.partial(_prompt_kernel, parity, has_pre),
        grid=(BATCH, NT),
        in_specs=specs,
        out_specs=out_specs,
        out_shape=out_shapes,
        scratch_shapes=scratch,
        compiler_params=_cparams(2),
        name=f"prompt_layer_{'even' if parity == 0 else 'odd'}{'_pre' if has_pre else ''}",
    )(*args)


def _sample_kernel(parity, has_pre, *refs):
    it = iter(refs)
    if has_pre:
        x1p_ref, f_ref, modp_ref, lpg_ref, lpb_ref = (next(it) for _ in range(5))
    else:
        x_ref = next(it)
    mod_ref, w_in_ref = next(it), next(it)
    if parity == 0:
        (hista_ref, wdw_ref, bdw_ref, lag_ref, lab_ref, lvg_ref, lvb_ref,
         wsp_ref, bsp_ref) = (next(it) for _ in range(9))
    else:
        histc_ref, histd_ref, wc_ref, wpool_ref, spool_ref = (next(it) for _ in range(5))
    w_out_ref, lng_ref, lnb_ref = (next(it) for _ in range(3))
    x1_ref, h2f_ref, h2b_ref = (next(it) for _ in range(3))
    if parity == 0:
        conva_ref, vnew_ref = next(it), next(it)
    else:
        convc_ref, poold_ref = next(it), next(it)

    def rows3(a):
        return a.reshape(DEC_SEQ, SB, a.shape[-1])

    def rows2(a):
        return a.reshape(SROWS, a.shape[-1])

    if has_pre:
        pre = DN_ALPHA * rows3(x1p_ref[...]) + modp_ref[5][None] * rows3(f_ref[...])
        x = _layer_norm(pre, lpg_ref[...], lpb_ref[...])
    else:
        x = rows3(x_ref[...])
    sh1, sc1, gt1, sh2, sc2 = (mod_ref[k][None] for k in range(5))
    h = x * (1.0 + sc1) + sh1
    z = rows3(_bdot(rows2(h), w_in_ref[...]))

    if parity == 0:
        a_val, a_gate = z[:, :, 0:HALF], z[:, :, HALF:2 * HALF]
        zu, zv = z[:, :, 2 * HALF:3 * HALF], z[:, :, 3 * HALF:4 * HALF]
        ga = a_val * _sigmoid(a_gate)
        n_hist = CONV_A - 1

        def ext(j):
            return hista_ref[0, j] if j < n_hist else ga[j - n_hist]

        steps = []
        for ts in range(DEC_SEQ):
            acc = jnp.zeros((SB, HALF), F32) + bdw_ref[...]
            for k in range(CONV_A):
                acc = acc + wdw_ref[k:k + 1, :] * ext(ts + k)
            steps.append(acc)
        conv = jnp.stack(steps, axis=0)
        y_lo = _silu(_layer_norm(conv, lag_ref[...], lab_ref[...]))
        for j in range(n_hist):
            conva_ref[0, j] = ext(j + DEC_SEQ)

        u = _gelu(zu)
        v = _layer_norm(_gelu(zv), lvg_ref[...], lvb_ref[...])
        vnew_ref[0] = v
        steps = []
        for ts in range(DEC_SEQ):
            acc = jnp.zeros((SB, HALF), F32) + bsp_ref[ts:ts + 1, :]
            for j in range(ts + 1):
                acc = acc + wsp_ref[ts, j:j + 1, :] * v[j]
            steps.append(acc)
        y_hi = u * jnp.stack(steps, axis=0)
    else:
        b_g, c_g = z[:, :, 0:HALF], z[:, :, HALF:2 * HALF]
        xt, p = z[:, :, 2 * HALF:3 * HALF], z[:, :, 3 * HALF:4 * HALF]
        cx = c_g * xt
        n_hc = CONV_C - 1

        def extc(j):
            return histc_ref[0, j] if j < n_hc else cx[j - n_hc]

        def extd(j):
            return histd_ref[0, j] if j < POOL_BUF else p[j - POOL_BUF]

        steps = []
        for ts in range(DEC_SEQ):
            acc = jnp.zeros((SB, HALF), F32)
            for k in range(CONV_C):
                acc = acc + wc_ref[k:k + 1, :] * extc(ts + k)
            steps.append(acc)
        y_lo = b_g * jnp.stack(steps, axis=0)
        for j in range(n_hc):
            convc_ref[0, j] = extc(j + DEC_SEQ)
        for j in range(POOL_BUF):
            poold_ref[0, j] = extd(j + DEC_SEQ)

        groups = []
        for g, w in enumerate(POOL_WINDOWS):
            lanes = slice(g * CHUNK, (g + 1) * CHUNK)
            steps = []
            for ts in range(DEC_SEQ):
                acc = extd(POOL_BUF + ts)[:, lanes]
                for j in range(1, w):
                    acc = acc + extd(POOL_BUF + ts - j)[:, lanes]
                cnt = float(min(w, PAST_LEN + ts + 1))
                steps.append(acc / cnt - p[ts][:, lanes])
            pooled = jnp.stack(steps, axis=0)
            groups.append(rows3(_bdot(rows2(pooled), wpool_ref[g])))
        y_hi = jnp.concatenate(groups, axis=2) * spool_ref[...]

    x1, h2 = _mixer_tail(rows2(x), rows2(y_lo), rows2(y_hi), w_out_ref, rows2(jnp.broadcast_to(gt1, x.shape)),
                         lng_ref[...], lnb_ref[...], rows2(jnp.broadcast_to(sc2, x.shape)),
                         rows2(jnp.broadcast_to(sh2, x.shape)))
    x1_ref[...] = x1
    h2f_ref[...] = h2
    h2b_ref[...] = h2.astype(BF16)


def _full1(shape):
    n = len(shape)
    return pl.BlockSpec(shape, lambda s: (0,) * n)


def _sample_layer(parity, x_or_pre, mod_s, w_in, hists, mix_w, w_out, lng, lnb):
    has_pre = isinstance(x_or_pre, tuple)
    rows = pl.BlockSpec((SROWS, D_MODEL), lambda s: (s, 0))
    f_rows = pl.BlockSpec((SROWS, D_MODEL), lambda s: (N_PROMPT // SROWS + s, 0))
    mod_spec = pl.BlockSpec((6, SB, D_MODEL), lambda s: (0, s, 0))
    vec_d = _full1((1, D_MODEL))
    vec_h = _full1((1, HALF))

    def hist_spec(n):
        return pl.BlockSpec((1, n, SB, HALF), lambda s: (s, 0, 0, 0))

    if has_pre:
        x1p, f_all, modp, lpg, lpb = x_or_pre
        args = [x1p, f_all, modp, lpg, lpb]
        specs = [rows, f_rows, mod_spec, vec_d, vec_d]
    else:
        args = [x_or_pre]
        specs = [rows]
    args += [mod_s, w_in]
    specs += [mod_spec, _full1((D_MODEL, 4 * HALF))]
    if parity == 0:
        wdw, bdw, lag, lab, lvg, lvb, wsp, bsp = mix_w
        args += [hists[0], wdw, bdw, lag, lab, lvg, lvb, wsp, bsp]
        specs += [hist_spec(CONV_A - 1), _full1((CONV_A, HALF)), vec_h, vec_h, vec_h, vec_h, vec_h,
                  _full1((DEC_SEQ, DEC_SEQ, HALF)), _full1((DEC_SEQ, HALF))]
        state_shapes = [jax.ShapeDtypeStruct((NSB, CONV_A - 1, SB, HALF), F32),
                        jax.ShapeDtypeStruct((NSB, DEC_SEQ, SB, HALF), F32)]
        state_specs = [hist_spec(CONV_A - 1), hist_spec(DEC_SEQ)]
    else:
        wc, wpool, spool = mix_w
        args += [hists[0], hists[1], wc, wpool, spool]
        specs += [hist_spec(CONV_C - 1), hist_spec(POOL_BUF), _full1((CONV_C, HALF)),
                  _full1((G_B, CHUNK, CHUNK)), vec_h]
        state_shapes = [jax.ShapeDtypeStruct((NSB, CONV_C - 1, SB, HALF), F32),
                        jax.ShapeDtypeStruct((NSB, POOL_BUF, SB, HALF), F32)]
        state_specs = [hist_spec(CONV_C - 1), hist_spec(POOL_BUF)]
    args += [w_out, lng, lnb]
    specs += [_full1((2 * HALF, D_MODEL)), vec_d, vec_d]
    out_shapes = [jax.ShapeDtypeStruct((N_SAMPLE, D_MODEL), F32),
                  jax.ShapeDtypeStruct((N_SAMPLE, D_MODEL), F32),
                  jax.ShapeDtypeStruct((N_SAMPLE, D_MODEL), BF16)] + state_shapes
    out_specs = [rows, rows, rows] + state_specs
    return pl.pallas_call(
        functools.partial(_sample_kernel, parity, has_pre),
        grid=(NSB,),
        in_specs=specs,
        out_specs=out_specs,
        out_shape=out_shapes,
        compiler_params=_cparams(1),
        name=f"sample_layer_{'even' if parity == 0 else 'odd'}{'_pre' if has_pre else ''}",
    )(*args)


def _final_kernel(is_sample, x1p_ref, f_ref, modp_ref, lpg_ref, lpb_ref, o_ref):
    if is_sample:
        x1p = x1p_ref[...].reshape(DEC_SEQ, SB, D_MODEL)
        f = f_ref[...].reshape(DEC_SEQ, SB, D_MODEL)
        x = _layer_norm(DN_ALPHA * x1p + modp_ref[5][None] * f, lpg_ref[...], lpb_ref[...])
        o_ref[...] = x.reshape(SROWS, D_MODEL)
    else:
        o_ref[...] = _layer_norm(DN_ALPHA * x1p_ref[...] + modp_ref[5, 0] * f_ref[...], lpg_ref[...], lpb_ref[...])


def _final_norm(is_sample, x1p, f_all, modp, lpg, lpb):
    if is_sample:
        grid = (NSB,)
        rows = pl.BlockSpec((SROWS, D_MODEL), lambda s: (s, 0))
        f_rows = pl.BlockSpec((SROWS, D_MODEL), lambda s: (N_PROMPT // SROWS + s, 0))
        mod_spec = pl.BlockSpec((6, SB, D_MODEL), lambda s: (0, s, 0))
        vec_d = _full1((1, D_MODEL))
        n = N_SAMPLE
    else:
        grid = (BATCH, NT)
        rows = pl.BlockSpec((TT, D_MODEL), lambda b, t: (b * NT + t, 0))
        f_rows = rows
        mod_spec = pl.BlockSpec((6, 1, 1, D_MODEL), lambda b, t: (0, b, 0, 0))
        vec_d = _full((1, D_MODEL))
        n = N_PROMPT
    return pl.pallas_call(
        functools.partial(_final_kernel, is_sample),
        grid=grid,
        in_specs=[rows, f_rows, mod_spec, vec_d, vec_d],
        out_specs=rows,
        out_shape=jax.ShapeDtypeStruct((n, D_MODEL), F32),
        compiler_params=_cparams(len(grid)),
        name=f"final_norm_{'sample' if is_sample else 'prompt'}",
    )(x1p, f_all, modp, lpg, lpb)


def _router_kernel(h_ref, wr_ref, br_ref, topi_ref, gate_ref, rank_ref, cnt_ref, tri_ref, carry_ref):
    i = pl.program_id(0)

    @pl.when(i == 0)
    def _():
        r = lax.broadcasted_iota(jnp.int32, (TM_R, TM_R), 0)
        c = lax.broadcasted_iota(jnp.int32, (TM_R, TM_R), 1)
        tri_ref[...] = jnp.where(r < c, 1.0, 0.0).astype(BF16)
        carry_ref[...] = jnp.zeros((N_EXPERTS, 1), F32)

    logits = lax.dot_general(wr_ref[...], h_ref[...], (((1,), (1,)), ((), ())),
                             precision=lax.Precision.HIGHEST, preferred_element_type=F32) + br_ref[...]
    eidx = lax.broadcasted_iota(jnp.int32, (N_EXPERTS, TM_R), 0).astype(F32)
    work = logits
    vals, idxs, hots = [], [], []
    for _ in range(TOP_K):
        m = jnp.max(work, axis=0, keepdims=True)
        sel = jnp.min(jnp.where(work == m, eidx, float(N_EXPERTS)), axis=0, keepdims=True)
        hot = eidx == sel
        vals.append(m)
        idxs.append(sel)
        hots.append(hot)
        work = jnp.where(hot, -jnp.inf, work)
    exps = [jnp.exp(v - vals[0]) for v in vals]
    denom = exps[0] + exps[1] + exps[2] + exps[3]
    chosen = jnp.where(hots[0] | hots[1] | hots[2] | hots[3], 1.0, 0.0)
    before = jnp.dot(chosen.astype(BF16), tri_ref[...], preferred_element_type=F32) + carry_ref[...]
    carry_ref[...] = carry_ref[...] + jnp.sum(chosen, axis=1, keepdims=True)
    ranks = [jnp.sum(jnp.where(hot, before, 0.0), axis=0, keepdims=True) for hot in hots]
    topi_ref[...] = jnp.concatenate(idxs, axis=0).astype(jnp.int32)
    gate_ref[...] = jnp.concatenate([e / denom for e in exps], axis=0)
    rank_ref[...] = jnp.concatenate(ranks, axis=0).astype(jnp.int32)
    cnt_ref[...] = carry_ref[...].astype(jnp.int32)


def _router(h2f, w_router_t, b_router):
    tok = pl.BlockSpec((TOP_K, TM_R), lambda i: (0, i))
    return pl.pallas_call(
        _router_kernel,
        grid=(N_TOK // TM_R,),
        in_specs=[
            pl.BlockSpec((TM_R, D_MODEL), lambda i: (i, 0)),
            pl.BlockSpec((N_EXPERTS, D_MODEL), lambda i: (0, 0)),
            pl.BlockSpec((N_EXPERTS, 1), lambda i: (0, 0)),
        ],
        out_specs=[tok, tok, tok, pl.BlockSpec((N_EXPERTS, 1), lambda i: (0, 0))],
        out_shape=[jax.ShapeDtypeStruct((TOP_K, N_TOK), jnp.int32),
                   jax.ShapeDtypeStruct((TOP_K, N_TOK), F32),
                   jax.ShapeDtypeStruct((TOP_K, N_TOK), jnp.int32),
                   jax.ShapeDtypeStruct((N_EXPERTS, 1), jnp.int32)],
        scratch_shapes=[pltpu.VMEM((TM_R, TM_R), BF16), pltpu.VMEM((N_EXPERTS, 1), F32)],
        compiler_params=_cparams(1),
        name="router_top4",
    )(h2f, w_router_t, b_router)


def _expert_kernel(te_ref, nu_ref, x_ref, w1_ref, b1_ref, w2_ref, b2_ref, o_ref, w1b_ref, w2b_ref):
    i = pl.program_id(0)
    used = i < nu_ref[0]
    fresh = jnp.logical_or(i == 0, te_ref[i] != te_ref[jnp.maximum(i - 1, 0)])

    @pl.when(jnp.logical_and(used, fresh))
    def _():
        w1b_ref[...] = w1_ref[0].astype(BF16)
        w2b_ref[...] = w2_ref[0].astype(BF16)

    @pl.when(used)
    def _():
        hdn = jnp.dot(x_ref[...], w1b_ref[...], preferred_element_type=F32) + b1_ref[0]
        glu = jnp.minimum(hdn[:, 0:D_FF], SWIGLU_LIMIT)
        lin = jnp.clip(hdn[:, D_FF:2 * D_FF], -SWIGLU_LIMIT, SWIGLU_LIMIT)
        act = glu * _sigmoid(SWIGLU_ALPHA * glu) * (lin + 1.0)
        o_ref[...] = jnp.dot(act.astype(BF16), w2b_ref[...], preferred_element_type=F32) + b2_ref[0]


def _experts(tile_e, n_used, xs, w1, b1, w2, b2):
    grid_spec = pltpu.PrefetchScalarGridSpec(
        num_scalar_prefetch=2,
        grid=(N_XTILES,),
        in_specs=[
            pl.BlockSpec((TM_X, D_MODEL), lambda i, te, nu: (i, 0)),
            pl.BlockSpec((1, D_MODEL, 2 * D_FF), lambda i, te, nu: (te[i], 0, 0)),
            pl.BlockSpec((1, 1, 2 * D_FF), lambda i, te, nu: (te[i], 0, 0)),
            pl.BlockSpec((1, D_FF, D_MODEL), lambda i, te, nu: (te[i], 0, 0)),
            pl.BlockSpec((1, 1, D_MODEL), lambda i, te, nu: (te[i], 0, 0)),
        ],
        out_specs=pl.BlockSpec((TM_X, D_MODEL), lambda i, te, nu: (i, 0)),
        scratch_shapes=[pltpu.VMEM((D_MODEL, 2 * D_FF), BF16), pltpu.VMEM((D_FF, D_MODEL), BF16)],
    )
    return pl.pallas_call(
        _expert_kernel,
        grid_spec=grid_spec,
        out_shape=jax.ShapeDtypeStruct((N_XROWS, D_MODEL), F32),
        compiler_params=_cparams(1),
        name="expert_ffn",
    )(tile_e, n_used, xs, w1, b1.reshape(N_EXPERTS, 1, 2 * D_FF), w2, b2.reshape(N_EXPERTS, 1, D_MODEL))


def _moe(h2f, h2b, w_router_t, b_router, w1, b1, w2, b2):
    top_i, gates, rank, counts = _router(h2f, w_router_t, b_router)
    counts = counts[:, 0]
    padded = (counts + TM_X - 1) // TM_X * TM_X
    pad_end = jnp.cumsum(padded)
    pad_start = pad_end - padded
    dest = pad_start[top_i] + rank
    tile_e = jnp.minimum(jnp.searchsorted(pad_end, jnp.arange(N_XTILES, dtype=jnp.int32) * TM_X, side="right"),
                         N_EXPERTS - 1).astype(jnp.int32)
    n_used = (pad_end[-1:] // TM_X).astype(jnp.int32)
    tok = jnp.tile(jnp.arange(N_TOK, dtype=jnp.int32), TOP_K)
    src = jnp.zeros((N_XROWS,), jnp.int32).at[dest.reshape(-1)].set(tok)
    xs = jnp.take(h2b, src, axis=0)
    ys = _experts(tile_e, n_used, xs, w1, b1, w2, b2)
    f = gates[0][:, None] * jnp.take(ys, dest[0], axis=0)
    for k in range(1, TOP_K):
        f = f + gates[k][:, None] * jnp.take(ys, dest[k], axis=0)
    return f


def _to_blocks(a):
    n, c = a.shape[1], a.shape[2]
    return a.reshape(NSB, SB, n, c).transpose(0, 2, 1, 3)


def _from_blocks(a):
    n, c = a.shape[1], a.shape[3]
    return a.transpose(0, 2, 1, 3).reshape(DEC_BATCH, n, c)


def kernel(x_prompt, x_sample, state_conv_a, state_conv_c, state_pool_d, c_prompt, c_sample,
           w_ada, b_ada, ln_g, ln_b,
           w_in_even, w_dw_a, b_dw_a, ln_a_g, ln_a_b, ln_v_g, ln_v_b, w_spatial, b_spatial, w_out_even,
           w_in_odd, w_conv_c, w_pool_d, scale_pool_d, w_out_odd,
           w_router, b_router, w_exp_in, b_exp_in, w_exp_out, b_exp_out):
    mod = _modulation(jnp.concatenate([c_prompt, c_sample], axis=0), w_ada, b_ada)
    mod_p = mod[:, :, :BATCH].reshape(DEPTH, 6, BATCH, 1, D_MODEL)
    mod_s = mod[:, :, BATCH:]

    xp = x_prompt.reshape(N_PROMPT, D_MODEL)
    xs_ = _to_blocks(x_sample).reshape(N_SAMPLE, D_MODEL)
    pre_p = xp
    pre_s = xs_
    new_a_p, new_a_s, new_v_s, new_c_p, new_c_s, new_d_p, new_d_s = [], [], [], [], [], [], []
    for l in range(DEPTH):
        i = l // 2
        lng, lnb = ln_g[l, 0][None], ln_b[l, 0][None]
        if l % 2 == 0:
            w_in = w_in_even[i].astype(BF16)
            w_out = w_out_even[i].astype(BF16)
            vecs = (w_dw_a[i], b_dw_a[i][None], ln_a_g[i][None], ln_a_b[i][None], ln_v_g[i][None], ln_v_b[i][None])
            mix_p = vecs + (w_spatial[i], b_spatial[i][:, :, None])
            wsp_s = jnp.repeat(w_spatial[i][:, :DEC_SEQ, :DEC_SEQ].transpose(1, 2, 0), CHUNK, axis=-1)
            bsp_s = jnp.repeat(b_spatial[i][:, :DEC_SEQ].T, CHUNK, axis=-1)
            mix_s = vecs + (wsp_s, bsp_s)
            hists = (_to_blocks(state_conv_a[i]),)
        else:
            w_in = w_in_odd[i].astype(BF16)
            w_out = w_out_odd[i].astype(BF16)
            mix_p = mix_s = (w_conv_c[i], w_pool_d[i], scale_pool_d[i][None])
            hists = (_to_blocks(state_conv_c[i]), _to_blocks(state_pool_d[i]))
        outs_p = _prompt_layer(l % 2, pre_p, mod_p[l], w_in, mix_p, w_out, lng, lnb)
        outs_s = _sample_layer(l % 2, pre_s, mod_s[l], w_in, hists, mix_s, w_out, lng, lnb)
        x1_p, h2f_p, h2b_p = outs_p[:3]
        x1_s, h2f_s, h2b_s = outs_s[:3]
        if l % 2 == 0:
            new_a_p.append(outs_p[3])
            new_a_s.append(_from_blocks(outs_s[3]))
            new_v_s.append(_from_blocks(outs_s[4]))
        else:
            new_c_p.append(outs_p[3])
            new_d_p.append(outs_p[4])
            new_c_s.append(_from_blocks(outs_s[3]))
            new_d_s.append(_from_blocks(outs_s[4]))
        f_all = _moe(jnp.concatenate([h2f_p, h2f_s], axis=0), jnp.concatenate([h2b_p, h2b_s], axis=0),
                     w_router[l].T, b_router[l][:, None], w_exp_in[l], b_exp_in[l], w_exp_out[l], b_exp_out[l])
        lpg, lpb = ln_g[l, 1][None], ln_b[l, 1][None]
        pre_p = (x1_p, f_all, mod_p[l], lpg, lpb)
        pre_s = (x1_s, f_all, mod_s[l], lpg, lpb)
    y_p = _final_norm(False, *pre_p).reshape(BATCH, SEQ, D_MODEL)
    y_s = _from_blocks(_final_norm(True, *pre_s).reshape(NSB, DEC_SEQ, SB, D_MODEL))
    return (y_p, y_s, jnp.stack(new_a_p), jnp.stack(new_a_s), jnp.stack(new_v_s),
            jnp.stack(new_c_p), jnp.stack(new_c_s), jnp.stack(new_d_p), jnp.stack(new_d_s))
```

```python
import functools

import jax
import jax.numpy as jnp
from jax import lax
from jax.experimental import pallas as pl
from jax.experimental.pallas import tpu as pltpu

F32 = jnp.float32
BF16 = jnp.bfloat16

D_MODEL = 1024
BATCH = 8
SEQ = 2048
DEPTH = 4
DEC_BATCH = 128
DEC_SEQ = 8
PAST_LEN = 16384
HALF = 512
CONV_A = 31
CHUNK = 128
G_B = 4
CONV_C = 3
POOL_WINDOWS = (2, 4, 8, 16)
POOL_BUF = 15
N_EXPERTS = 32
TOP_K = 4
D_FF = D_MODEL
SWIGLU_LIMIT = 7.0
SWIGLU_ALPHA = 1.702
DN_ALPHA = (2 * DEPTH) ** 0.25
LN_EPS = 1e-5

N_PROMPT = BATCH * SEQ
N_SAMPLE = DEC_BATCH * DEC_SEQ
N_TOK = N_PROMPT + N_SAMPLE

TT = 256
NT = SEQ // TT
SB = 32
NSB = DEC_BATCH // SB
SROWS = SB * DEC_SEQ
HALO_A = 32
HALO_C = 8
HALO_D = 16
CONV_ROWS = 32

SUBLANES = 8
LANES = 128
TOK_TILE = 256
N_TTILES = N_TOK // TOK_TILE
TM_X = 256
N_ASSIGN = N_TOK * TOP_K
DMA_ROWS = 8
SLACK = DMA_ROWS - 1
LBUF = 1280
CAP = N_TOK + TM_X
CAP_TILES = CAP // TM_X
N_XTILES = (N_ASSIGN + N_EXPERTS * (TM_X - 1 + SLACK)) // TM_X + 1
ROW_TILE = (SUBLANES, LANES)

VMEM_LIMIT = 56 * 1024 * 1024


def _cparams(n_axes):
    return pltpu.CompilerParams(dimension_semantics=("arbitrary",) * n_axes, vmem_limit_bytes=VMEM_LIMIT)


def _layer_norm(x, g, b):
    mu = jnp.mean(x, axis=-1, keepdims=True)
    xc = x - mu
    var = jnp.mean(xc * xc, axis=-1, keepdims=True)
    return xc * lax.rsqrt(var + LN_EPS) * g + b


def _sigmoid(x):
    return jax.nn.sigmoid(x)


def _silu(x):
    return x * _sigmoid(x)


def _gelu(x):
    return 0.5 * x * (1.0 + lax.erf(x * (0.5 ** 0.5)))


def _bdot(a, b):
    return jnp.dot(a.astype(BF16), b.astype(BF16), preferred_element_type=F32)


def _mod_kernel(c_ref, w_ref, b_ref, o_ref):
    c = c_ref[...]
    o_ref[0, 0] = _bdot(_silu(c), w_ref[0]) + b_ref[0]


def _modulation(c_all, w_ada, b_ada):
    nb = c_all.shape[0]
    return pl.pallas_call(
        _mod_kernel,
        grid=(DEPTH, 6),
        in_specs=[
            pl.BlockSpec((nb, D_MODEL), lambda l, k: (0, 0)),
            pl.BlockSpec((1, D_MODEL, D_MODEL), lambda l, k: (l, 0, k)),
            pl.BlockSpec((1, 1, D_MODEL), lambda l, k: (l, 0, k)),
        ],
        out_specs=pl.BlockSpec((1, 1, nb, D_MODEL), lambda l, k: (l, k, 0, 0)),
        out_shape=jax.ShapeDtypeStruct((DEPTH, 6, nb, D_MODEL), F32),
        compiler_params=_cparams(2),
        name="adaln_modulation",
    )(c_all, w_ada, b_ada.reshape(DEPTH, 1, 6 * D_MODEL))


def _mixer_tail(x, y_lo, y_hi, w_out_ref, gt1, lng, lnb, sc2, sh2):
    y = _bdot(y_lo, w_out_ref[0:HALF, :]) + _bdot(y_hi, w_out_ref[HALF:2 * HALF, :])
    x1 = _layer_norm(DN_ALPHA * x + gt1 * y, lng, lnb)
    h2 = x1 * (1.0 + sc2) + sh2
    return x1, h2


def _carry_halo(ext_ref, halo, t):
    @pl.when(t == 0)
    def _():
        ext_ref[0:halo, :] = jnp.zeros((halo, HALF), F32)

    @pl.when(t > 0)
    def _():
        ext_ref[0:halo, :] = ext_ref[TT:TT + halo, :]


def _prompt_kernel(parity, has_pre, *refs):
    it = iter(refs)
    if has_pre:
        x1p_ref, f_ref, modp_ref, lpg_ref, lpb_ref = (next(it) for _ in range(5))
    else:
        x_ref = next(it)
    mod_ref, w_in_ref = next(it), next(it)
    if parity == 0:
        wdw_ref, bdw_ref, lag_ref, lab_ref, lvg_ref, lvb_ref, wsp_ref, bsp_ref = (next(it) for _ in range(8))
    else:
        wc_ref, wpool_ref, spool_ref = (next(it) for _ in range(3))
    w_out_ref, lng_ref, lnb_ref = (next(it) for _ in range(3))
    x1_ref, h2f_ref, h2b_ref = (next(it) for _ in range(3))
    if parity == 0:
        conva_ref, ext_ref = next(it), next(it)
    else:
        convc_ref, poold_ref, extc_ref, extd_ref = (next(it) for _ in range(4))

    t = pl.program_id(1)
    if has_pre:
        x = _layer_norm(DN_ALPHA * x1p_ref[...] + modp_ref[5, 0] * f_ref[...], lpg_ref[...], lpb_ref[...])
    else:
        x = x_ref[...]
    sh1, sc1, gt1, sh2, sc2 = (mod_ref[k, 0] for k in range(5))
    h = x * (1.0 + sc1) + sh1
    z = _bdot(h, w_in_ref[...])

    if parity == 0:
        a_val, a_gate = z[:, 0:HALF], z[:, HALF:2 * HALF]
        zu, zv = z[:, 2 * HALF:3 * HALF], z[:, 3 * HALF:4 * HALF]
        _carry_halo(ext_ref, HALO_A, t)
        ext_ref[HALO_A:HALO_A + TT, :] = a_val * _sigmoid(a_gate)
        bias = bdw_ref[...]
        parts = []
        for r in range(TT // CONV_ROWS):
            acc = jnp.zeros((CONV_ROWS, HALF), F32) + bias
            for k in range(CONV_A):
                off = r * CONV_ROWS + k + (HALO_A - CONV_A + 1)
                acc = acc + wdw_ref[k:k + 1, :] * ext_ref[off:off + CONV_ROWS, :]
            parts.append(acc)
        conv = jnp.concatenate(parts, axis=0)
        y_lo = _silu(_layer_norm(conv, lag_ref[...], lab_ref[...]))

        @pl.when(t == NT - 1)
        def _():
            conva_ref[0] = ext_ref[TT + HALO_A - (CONV_A - 1):TT + HALO_A, :]

        u = _gelu(zu)
        v = _layer_norm(_gelu(zv), lvg_ref[...], lvb_ref[...])
        row = lax.broadcasted_iota(jnp.int32, (CHUNK, CHUNK), 0)
        col = lax.broadcasted_iota(jnp.int32, (CHUNK, CHUNK), 1)
        chunks = []
        for c in range(TT // CHUNK):
            groups = []
            for g in range(G_B):
                wg = jnp.where(row >= col, wsp_ref[g], 0.0)
                vg = v[c * CHUNK:(c + 1) * CHUNK, g * CHUNK:(g + 1) * CHUNK]
                groups.append(_bdot(wg, vg) + bsp_ref[g])
            chunks.append(jnp.concatenate(groups, axis=1))
        y_hi = u * jnp.concatenate(chunks, axis=0)
    else:
        b_g, c_g = z[:, 0:HALF], z[:, HALF:2 * HALF]
        xt, p = z[:, 2 * HALF:3 * HALF], z[:, 3 * HALF:4 * HALF]
        _carry_halo(extc_ref, HALO_C, t)
        extc_ref[HALO_C:HALO_C + TT, :] = c_g * xt
        conv = jnp.zeros((TT, HALF), F32)
        for k in range(CONV_C):
            off = k + (HALO_C - CONV_C + 1)
            conv = conv + wc_ref[k:k + 1, :] * extc_ref[off:off + TT, :]
        y_lo = b_g * conv
        _carry_halo(extd_ref, HALO_D, t)
        extd_ref[HALO_D:HALO_D + TT, :] = p
        pos = t * TT + lax.broadcasted_iota(jnp.int32, (TT, CHUNK), 0)
        groups = []
        for g, w in enumerate(POOL_WINDOWS):
            lanes = slice(g * CHUNK, (g + 1) * CHUNK)
            acc = extd_ref[HALO_D:HALO_D + TT, lanes]
            for j in range(1, w):
                acc = acc + extd_ref[HALO_D - j:HALO_D - j + TT, lanes]
            cnt = jnp.minimum(w, pos + 1).astype(F32)
            pooled = acc / cnt - p[:, lanes]
            groups.append(_bdot(pooled, wpool_ref[g]))
        y_hi = jnp.concatenate(groups, axis=1) * spool_ref[...]

        @pl.when(t == NT - 1)
        def _():
            convc_ref[0] = extc_ref[TT + HALO_C - (CONV_C - 1):TT + HALO_C, :]
            poold_ref[0] = extd_ref[TT + HALO_D - POOL_BUF:TT + HALO_D, :]

    x1, h2 = _mixer_tail(x, y_lo, y_hi, w_out_ref, gt1, lng_ref[...], lnb_ref[...], sc2, sh2)
    x1_ref[...] = x1
    h2f_ref[...] = h2
    h2b_ref[...] = h2.astype(BF16)


def _full(shape):
    n = len(shape)
    return pl.BlockSpec(shape, lambda b, t: (0,) * n)


def _prompt_layer(parity, x_or_pre, mod_p, w_in, mix_w, w_out, lng, lnb):
    has_pre = isinstance(x_or_pre, tuple)
    rows = pl.BlockSpec((TT, D_MODEL), lambda b, t: (b * NT + t, 0))
    mod_spec = pl.BlockSpec((6, 1, 1, D_MODEL), lambda b, t: (0, b, 0, 0))
    vec_d = _full((1, D_MODEL))
    vec_h = _full((1, HALF))
    if has_pre:
        x1p, f_all, modp, lpg, lpb = x_or_pre
        args = [x1p, f_all, modp, lpg, lpb]
        specs = [rows, rows, mod_spec, vec_d, vec_d]
    else:
        args = [x_or_pre]
        specs = [rows]
    args += [mod_p, w_in]
    specs += [mod_spec, _full((D_MODEL, 4 * HALF))]
    if parity == 0:
        wdw, bdw, lag, lab, lvg, lvb, wsp, bsp = mix_w
        args += [wdw, bdw, lag, lab, lvg, lvb, wsp, bsp]
        specs += [_full((CONV_A, HALF)), vec_h, vec_h, vec_h, vec_h, vec_h,
                  _full((G_B, CHUNK, CHUNK)), _full((G_B, CHUNK, 1))]
        state_shapes = [jax.ShapeDtypeStruct((BATCH, CONV_A - 1, HALF), F32)]
        state_specs = [pl.BlockSpec((1, CONV_A - 1, HALF), lambda b, t: (b, 0, 0))]
        scratch = [pltpu.VMEM((HALO_A + TT, HALF), F32)]
    else:
        wc, wpool, spool = mix_w
        args += [wc, wpool, spool]
        specs += [_full((CONV_C, HALF)), _full((G_B, CHUNK, CHUNK)), vec_h]
        state_shapes = [jax.ShapeDtypeStruct((BATCH, CONV_C - 1, HALF), F32),
                        jax.ShapeDtypeStruct((BATCH, POOL_BUF, HALF), F32)]
        state_specs = [pl.BlockSpec((1, CONV_C - 1, HALF), lambda b, t: (b, 0, 0)),
                       pl.BlockSpec((1, POOL_BUF, HALF), lambda b, t: (b, 0, 0))]
        scratch = [pltpu.VMEM((HALO_C + TT, HALF), F32), pltpu.VMEM((HALO_D + TT, HALF), F32)]
    args += [w_out, lng, lnb]
    specs += [_full((2 * HALF, D_MODEL)), vec_d, vec_d]
    out_shapes = [jax.ShapeDtypeStruct((N_PROMPT, D_MODEL), F32),
                  jax.ShapeDtypeStruct((N_PROMPT, D_MODEL), F32),
                  jax.ShapeDtypeStruct((N_PROMPT, D_MODEL), BF16)] + state_shapes
    out_specs = [rows, rows, rows] + state_specs
    return pl.pallas_call(
        functools.partial(_prompt_kernel, parity, has_pre),
        grid=(BATCH, NT),
        in_specs=specs,
        out_specs=out_specs,
        out_shape=out_shapes,
        scratch_shapes=scratch,
        compiler_params=_cparams(2),
        name=f"prompt_layer_{'even' if parity == 0 else 'odd'}{'_pre' if has_pre else ''}",
    )(*args)


def _sample_kernel(parity, has_pre, *refs):
    it = iter(refs)
    if has_pre:
        x1p_ref, f_ref, modp_ref, lpg_ref, lpb_ref = (next(it) for _ in range(5))
    else:
        x_ref = next(it)
    mod_ref, w_in_ref = next(it), next(it)
    if parity == 0:
        (hista_ref, wdw_ref, bdw_ref, lag_ref, lab_ref, lvg_ref, lvb_ref,
         wsp_ref, bsp_ref) = (next(it) for _ in range(9))
    else:
        histc_ref, histd_ref, wc_ref, wpool_ref, spool_ref = (next(it) for _ in range(5))
    w_out_ref, lng_ref, lnb_ref = (next(it) for _ in range(3))
    x1_ref, h2f_ref, h2b_ref = (next(it) for _ in range(3))
    if parity == 0:
        conva_ref, vnew_ref = next(it), next(it)
    else:
        convc_ref, poold_ref = next(it), next(it)

    def rows3(a):
        return a.reshape(DEC_SEQ, SB, a.shape[-1])

    def rows2(a):
        return a.reshape(SROWS, a.shape[-1])

    if has_pre:
        pre = DN_ALPHA * rows3(x1p_ref[...]) + modp_ref[5][None] * rows3(f_ref[...])
        x = _layer_norm(pre, lpg_ref[...], lpb_ref[...])
    else:
        x = rows3(x_ref[...])
    sh1, sc1, gt1, sh2, sc2 = (mod_ref[k][None] for k in range(5))
    h = x * (1.0 + sc1) + sh1
    z = rows3(_bdot(rows2(h), w_in_ref[...]))

    if parity == 0:
        a_val, a_gate = z[:, :, 0:HALF], z[:, :, HALF:2 * HALF]
        zu, zv = z[:, :, 2 * HALF:3 * HALF], z[:, :, 3 * HALF:4 * HALF]
        ga = a_val * _sigmoid(a_gate)
        n_hist = CONV_A - 1

        def ext(j):
            return hista_ref[0, j] if j < n_hist else ga[j - n_hist]

        steps = []
        for ts in range(DEC_SEQ):
            acc = jnp.zeros((SB, HALF), F32) + bdw_ref[...]
            for k in range(CONV_A):
                acc = acc + wdw_ref[k:k + 1, :] * ext(ts + k)
            steps.append(acc)
        conv = jnp.stack(steps, axis=0)
        y_lo = _silu(_layer_norm(conv, lag_ref[...], lab_ref[...]))
        for j in range(n_hist):
            conva_ref[0, j] = ext(j + DEC_SEQ)

        u = _gelu(zu)
        v = _layer_norm(_gelu(zv), lvg_ref[...], lvb_ref[...])
        vnew_ref[0] = v
        steps = []
        for ts in range(DEC_SEQ):
            acc = jnp.zeros((SB, HALF), F32) + bsp_ref[ts:ts + 1, :]
            for j in range(ts + 1):
                acc = acc + wsp_ref[ts, j:j + 1, :] * v[j]
            steps.append(acc)
        y_hi = u * jnp.stack(steps, axis=0)
    else:
        b_g, c_g = z[:, :, 0:HALF], z[:, :, HALF:2 * HALF]
        xt, p = z[:, :, 2 * HALF:3 * HALF], z[:, :, 3 * HALF:4 * HALF]
        cx = c_g * xt
        n_hc = CONV_C - 1

        def extc(j):
            return histc_ref[0, j] if j < n_hc else cx[j - n_hc]

        def extd(j):
            return histd_ref[0, j] if j < POOL_BUF else p[j - POOL_BUF]

        steps = []
        for ts in range(DEC_SEQ):
            acc = jnp.zeros((SB, HALF), F32)
            for k in range(CONV_C):
                acc = acc + wc_ref[k:k + 1, :] * extc(ts + k)
            steps.append(acc)
        y_lo = b_g * jnp.stack(steps, axis=0)
        for j in range(n_hc):
            convc_ref[0, j] = extc(j + DEC_SEQ)
        for j in range(POOL_BUF):
            poold_ref[0, j] = extd(j + DEC_SEQ)

        groups = []
        for g, w in enumerate(POOL_WINDOWS):
            lanes = slice(g * CHUNK, (g + 1) * CHUNK)
            steps = []
            for ts in range(DEC_SEQ):
                acc = extd(POOL_BUF + ts)[:, lanes]
                for j in range(1, w):
                    acc = acc + extd(POOL_BUF + ts - j)[:, lanes]
                cnt = float(min(w, PAST_LEN + ts + 1))
                steps.append(acc / cnt - p[ts][:, lanes])
            pooled = jnp.stack(steps, axis=0)
            groups.append(rows3(_bdot(rows2(pooled), wpool_ref[g])))
        y_hi = jnp.concatenate(groups, axis=2) * spool_ref[...]

    x1, h2 = _mixer_tail(rows2(x), rows2(y_lo), rows2(y_hi), w_out_ref, rows2(jnp.broadcast_to(gt1, x.shape)),
                         lng_ref[...], lnb_ref[...], rows2(jnp.broadcast_to(sc2, x.shape)),
                         rows2(jnp.broadcast_to(sh2, x.shape)))
    x1_ref[...] = x1
    h2f_ref[...] = h2
    h2b_ref[...] = h2.astype(BF16)


def _full1(shape):
    n = len(shape)
    return pl.BlockSpec(shape, lambda s: (0,) * n)


def _sample_layer(parity, x_or_pre, mod_s, w_in, hists, mix_w, w_out, lng, lnb):
    has_pre = isinstance(x_or_pre, tuple)
    rows = pl.BlockSpec((SROWS, D_MODEL), lambda s: (s, 0))
    f_rows = pl.BlockSpec((SROWS, D_MODEL), lambda s: (N_PROMPT // SROWS + s, 0))
    mod_spec = pl.BlockSpec((6, SB, D_MODEL), lambda s: (0, s, 0))
    vec_d = _full1((1, D_MODEL))
    vec_h = _full1((1, HALF))

    def hist_spec(n):
        return pl.BlockSpec((1, n, SB, HALF), lambda s: (s, 0, 0, 0))

    if has_pre:
        x1p, f_all, modp, lpg, lpb = x_or_pre
        args = [x1p, f_all, modp, lpg, lpb]
        specs = [rows, f_rows, mod_spec, vec_d, vec_d]
    else:
        args = [x_or_pre]
        specs = [rows]
    args += [mod_s, w_in]
    specs += [mod_spec, _full1((D_MODEL, 4 * HALF))]
    if parity == 0:
        wdw, bdw, lag, lab, lvg, lvb, wsp, bsp = mix_w
        args += [hists[0], wdw, bdw, lag, lab, lvg, lvb, wsp, bsp]
        specs += [hist_spec(CONV_A - 1), _full1((CONV_A, HALF)), vec_h, vec_h, vec_h, vec_h, vec_h,
                  _full1((DEC_SEQ, DEC_SEQ, HALF)), _full1((DEC_SEQ, HALF))]
        state_shapes = [jax.ShapeDtypeStruct((NSB, CONV_A - 1, SB, HALF), F32),
                        jax.ShapeDtypeStruct((NSB, DEC_SEQ, SB, HALF), F32)]
        state_specs = [hist_spec(CONV_A - 1), hist_spec(DEC_SEQ)]
    else:
        wc, wpool, spool = mix_w
        args += [hists[0], hists[1], wc, wpool, spool]
        specs += [hist_spec(CONV_C - 1), hist_spec(POOL_BUF), _full1((CONV_C, HALF)),
                  _full1((G_B, CHUNK, CHUNK)), vec_h]
        state_shapes = [jax.ShapeDtypeStruct((NSB, CONV_C - 1, SB, HALF), F32),
                        jax.ShapeDtypeStruct((NSB, POOL_BUF, SB, HALF), F32)]
        state_specs = [hist_spec(CONV_C - 1), hist_spec(POOL_BUF)]
    args += [w_out, lng, lnb]
    specs += [_full1((2 * HALF, D_MODEL)), vec_d, vec_d]
    out_shapes = [jax.ShapeDtypeStruct((N_SAMPLE, D_MODEL), F32),
                  jax.ShapeDtypeStruct((N_SAMPLE, D_MODEL), F32),
                  jax.ShapeDtypeStruct((N_SAMPLE, D_MODEL), BF16)] + state_shapes
    out_specs = [rows, rows, rows] + state_specs
    return pl.pallas_call(
        functools.partial(_sample_kernel, parity, has_pre),
        grid=(NSB,),
        in_specs=specs,
        out_specs=out_specs,
        out_shape=out_shapes,
        compiler_params=_cparams(1),
        name=f"sample_layer_{'even' if parity == 0 else 'odd'}{'_pre' if has_pre else ''}",
    )(*args)


def _final_kernel(is_sample, x1p_ref, f_ref, modp_ref, lpg_ref, lpb_ref, o_ref):
    if is_sample:
        x1p = x1p_ref[...].reshape(DEC_SEQ, SB, D_MODEL)
        f = f_ref[...].reshape(DEC_SEQ, SB, D_MODEL)
        x = _layer_norm(DN_ALPHA * x1p + modp_ref[5][None] * f, lpg_ref[...], lpb_ref[...])
        o_ref[...] = x.reshape(SROWS, D_MODEL)
    else:
        o_ref[...] = _layer_norm(DN_ALPHA * x1p_ref[...] + modp_ref[5, 0] * f_ref[...], lpg_ref[...], lpb_ref[...])


def _final_norm(is_sample, x1p, f_all, modp, lpg, lpb):
    if is_sample:
        grid = (NSB,)
        rows = pl.BlockSpec((SROWS, D_MODEL), lambda s: (s, 0))
        f_rows = pl.BlockSpec((SROWS, D_MODEL), lambda s: (N_PROMPT // SROWS + s, 0))
        mod_spec = pl.BlockSpec((6, SB, D_MODEL), lambda s: (0, s, 0))
        vec_d = _full1((1, D_MODEL))
        n = N_SAMPLE
    else:
        grid = (BATCH, NT)
        rows = pl.BlockSpec((TT, D_MODEL), lambda b, t: (b * NT + t, 0))
        f_rows = rows
        mod_spec = pl.BlockSpec((6, 1, 1, D_MODEL), lambda b, t: (0, b, 0, 0))
        vec_d = _full((1, D_MODEL))
        n = N_PROMPT
    return pl.pallas_call(
        functools.partial(_final_kernel, is_sample),
        grid=grid,
        in_specs=[rows, f_rows, mod_spec, vec_d, vec_d],
        out_specs=rows,
        out_shape=jax.ShapeDtypeStruct((n, D_MODEL), F32),
        compiler_params=_cparams(len(grid)),
        name=f"final_norm_{'sample' if is_sample else 'prompt'}",
    )(x1p, f_all, modp, lpg, lpb)


def _router_kernel(h_ref, wr_ref, br_ref, gate_ref, lp_ref, tcnt_ref, tslot_ref, tdst_ref, cnt_ref,
                   tri_ref, low_ref, carry_ref):
    i = pl.program_id(0)

    @pl.when(i == 0)
    def _():
        r = lax.broadcasted_iota(jnp.int32, (TOK_TILE, TOK_TILE), 0)
        c = lax.broadcasted_iota(jnp.int32, (TOK_TILE, TOK_TILE), 1)
        tri_ref[...] = jnp.where(r < c, 1.0, 0.0).astype(BF16)
        r = lax.broadcasted_iota(jnp.int32, (N_EXPERTS, N_EXPERTS), 0)
        c = lax.broadcasted_iota(jnp.int32, (N_EXPERTS, N_EXPERTS), 1)
        low_ref[...] = jnp.where(c < r, 1.0, 0.0).astype(BF16)
        carry_ref[...] = jnp.zeros((N_EXPERTS, 1), F32)

    logits = lax.dot_general(wr_ref[...], h_ref[...], (((1,), (1,)), ((), ())),
                             precision=lax.Precision.HIGHEST, preferred_element_type=F32) + br_ref[...]
    eidx = lax.broadcasted_iota(jnp.int32, (N_EXPERTS, TOK_TILE), 0).astype(F32)
    work = logits
    vals, hots = [], []
    for _ in range(TOP_K):
        m = jnp.max(work, axis=0, keepdims=True)
        sel = jnp.min(jnp.where(work == m, eidx, float(N_EXPERTS)), axis=0, keepdims=True)
        hot = eidx == sel
        vals.append(m)
        hots.append(hot)
        work = jnp.where(hot, -jnp.inf, work)
    exps = [jnp.exp(v - vals[0]) for v in vals]
    denom = exps[0] + exps[1] + exps[2] + exps[3]
    chosen = jnp.where(hots[0] | hots[1] | hots[2] | hots[3], 1.0, 0.0)
    before = jnp.dot(chosen.astype(BF16), tri_ref[...], preferred_element_type=F32)
    cnt = jnp.sum(chosen, axis=1, keepdims=True)
    copies = jnp.floor((cnt + float(SLACK)) * (1.0 / DMA_ROWS))
    slot = float(DMA_ROWS) * jnp.dot(low_ref[...], jnp.broadcast_to(copies, (N_EXPERTS, TOK_TILE)).astype(BF16),
                                     preferred_element_type=F32)
    pos = slot + before
    lps = [jnp.sum(jnp.where(hot, pos, 0.0), axis=0, keepdims=True) for hot in hots]
    ecol = lax.broadcasted_iota(jnp.int32, (N_EXPERTS, 1), 0).astype(F32)
    gate_ref[...] = jnp.concatenate([e / denom for e in exps], axis=0)
    lp_ref[...] = jnp.concatenate(lps, axis=0).astype(jnp.int32)
    tcnt_ref[0] = cnt.astype(jnp.int32)
    tslot_ref[0] = slot[:, 0:1].astype(jnp.int32)
    tdst_ref[0] = (ecol * float(CAP) + carry_ref[...]).astype(jnp.int32)
    carry_ref[...] = carry_ref[...] + cnt
    cnt_ref[...] = carry_ref[...].astype(jnp.int32)


def _router(h2f, w_router_t, b_router):
    tok = pl.BlockSpec((TOP_K, TOK_TILE), lambda i: (0, i))
    tab = pl.BlockSpec((1, N_EXPERTS, 1), lambda i: (i, 0, 0))
    tab_shape = jax.ShapeDtypeStruct((N_TTILES, N_EXPERTS, 1), jnp.int32)
    return pl.pallas_call(
        _router_kernel,
        grid=(N_TTILES,),
        in_specs=[
            pl.BlockSpec((TOK_TILE, D_MODEL), lambda i: (i, 0)),
            pl.BlockSpec((N_EXPERTS, D_MODEL), lambda i: (0, 0)),
            pl.BlockSpec((N_EXPERTS, 1), lambda i: (0, 0)),
        ],
        out_specs=[tok, tok, tab, tab, tab, pl.BlockSpec((N_EXPERTS, 1), lambda i: (0, 0))],
        out_shape=[jax.ShapeDtypeStruct((TOP_K, N_TOK), F32),
                   jax.ShapeDtypeStruct((TOP_K, N_TOK), jnp.int32),
                   tab_shape, tab_shape, tab_shape,
                   jax.ShapeDtypeStruct((N_EXPERTS, 1), jnp.int32)],
        scratch_shapes=[pltpu.VMEM((TOK_TILE, TOK_TILE), BF16), pltpu.VMEM((N_EXPERTS, N_EXPERTS), BF16),
                        pltpu.VMEM((N_EXPERTS, 1), F32)],
        compiler_params=_cparams(1),
        name="router_top4",
    )(h2f, w_router_t, b_router)


def _segment_copies(tile, tcnt_ref, tslot_ref, tdst_ref, stage_ref, slot, big_ref, sem, to_big):
    for e in range(N_EXPERTS):
        idx = tile * N_EXPERTS + e
        n = lax.shift_right_logical(tcnt_ref[idx] + SLACK, 3)
        s0 = tslot_ref[idx]
        g0 = tdst_ref[idx]

        def body(j, carry, s0=s0, g0=g0):
            st = stage_ref.at[slot, pl.ds(pl.multiple_of((s0 + DMA_ROWS * j) * SUBLANES, DMA_ROWS * SUBLANES),
                                          DMA_ROWS * SUBLANES), :]
            bg = big_ref.at[pl.ds(pl.multiple_of((g0 + DMA_ROWS * j) * SUBLANES, SUBLANES),
                                  DMA_ROWS * SUBLANES), :]
            if to_big:
                pltpu.make_async_copy(st, bg, sem.at[slot]).start()
            else:
                pltpu.make_async_copy(bg, st, sem.at[slot]).start()
            return carry

        lax.fori_loop(0, n, body, 0)


def _wait_copies(n_copies, stage_ref, slot, big_ref, sem, to_big):
    st = stage_ref.at[slot, pl.ds(0, DMA_ROWS * SUBLANES), :]
    bg = big_ref.at[pl.ds(0, DMA_ROWS * SUBLANES), :]

    def body(j, carry):
        if to_big:
            pltpu.make_async_copy(st, bg, sem.at[slot]).wait()
        else:
            pltpu.make_async_copy(bg, st, sem.at[slot]).wait()
        return carry

    lax.fori_loop(0, n_copies, body, 0)


def _dispatch_kernel(tcnt_ref, tslot_ref, tdst_ref, tcop_ref, h_ref, lp_ref, xs_ref, stage_ref, sem):
    i = pl.program_id(0)
    slot = i % 2
    lp = lp_ref[...]
    q = lax.broadcasted_iota(jnp.int32, (LBUF, TOK_TILE), 0)
    hit = (q == lp[0:1, :]) | (q == lp[1:2, :]) | (q == lp[2:3, :]) | (q == lp[3:4, :])
    sort_t = jnp.where(hit, 1.0, 0.0).astype(BF16)
    h = h_ref[...]
    for c in range(LBUF // TOK_TILE):
        part = jnp.dot(sort_t[c * TOK_TILE:(c + 1) * TOK_TILE, :], h, preferred_element_type=F32)
        for s in range(SUBLANES):
            stage_ref[slot, pl.ds(c * TOK_TILE * SUBLANES + s, TOK_TILE, stride=SUBLANES), :] = (
                part[:, s * LANES:(s + 1) * LANES])

    @pl.when(i > 0)
    def _():
        _wait_copies(tcop_ref[i - 1], stage_ref, 1 - slot, xs_ref, sem, True)

    _segment_copies(i, tcnt_ref, tslot_ref, tdst_ref, stage_ref, slot, xs_ref, sem, True)

    @pl.when(i == N_TTILES - 1)
    def _():
        _wait_copies(tcop_ref[i], stage_ref, slot, xs_ref, sem, True)


def _dispatch(tabs, h2b, lp):
    grid_spec = pltpu.PrefetchScalarGridSpec(
        num_scalar_prefetch=4,
        grid=(N_TTILES,),
        in_specs=[
            pl.BlockSpec((TOK_TILE, D_MODEL), lambda i, *_: (i, 0)),
            pl.BlockSpec((TOP_K, TOK_TILE), lambda i, *_: (0, i)),
        ],
        out_specs=pl.BlockSpec(memory_space=pl.ANY),
        scratch_shapes=[pltpu.VMEM((2, LBUF * SUBLANES, LANES), F32), pltpu.SemaphoreType.DMA((2,))],
    )
    return pl.pallas_call(
        _dispatch_kernel,
        grid_spec=grid_spec,
        out_shape=jax.ShapeDtypeStruct((N_EXPERTS * CAP * SUBLANES, LANES), F32),
        compiler_params=_cparams(1),
        name="moe_dispatch",
    )(*tabs, h2b, lp)


def _combine_kernel(tcnt_ref, tslot_ref, tdst_ref, tcop_ref, lpt_ref, gt_ref, ys_ref, f_ref, stage_ref, sem):
    i = pl.program_id(0)
    slot = i % 2

    @pl.when(i == 0)
    def _():
        stage_ref[...] = jnp.zeros(stage_ref.shape, F32)
        _segment_copies(i, tcnt_ref, tslot_ref, tdst_ref, stage_ref, slot, ys_ref, sem, False)

    @pl.when(i + 1 < N_TTILES)
    def _():
        _segment_copies(i + 1, tcnt_ref, tslot_ref, tdst_ref, stage_ref, 1 - slot, ys_ref, sem, False)

    _wait_copies(tcop_ref[i], stage_ref, slot, ys_ref, sem, False)
    ysl = jnp.concatenate([stage_ref[slot, pl.ds(s, LBUF, stride=SUBLANES), :] for s in range(SUBLANES)],
                          axis=1).astype(BF16)
    q = lax.broadcasted_iota(jnp.int32, (TOK_TILE, LBUF), 1)
    lpt = lpt_ref[...]
    gt = gt_ref[...]
    w = jnp.zeros((TOK_TILE, LBUF), F32)
    for k in range(TOP_K):
        w = w + jnp.where(q == lpt[:, k:k + 1], gt[:, k:k + 1], 0.0)
    w_hi = w.astype(BF16)
    w_lo = (w - w_hi.astype(F32)).astype(BF16)
    f_ref[...] = (jnp.dot(w_hi, ysl, preferred_element_type=F32) +
                  jnp.dot(w_lo, ysl, preferred_element_type=F32))


def _combine(tabs, lp_t, gates_t, ys):
    grid_spec = pltpu.PrefetchScalarGridSpec(
        num_scalar_prefetch=4,
        grid=(N_TTILES,),
        in_specs=[
            pl.BlockSpec((TOK_TILE, TOP_K), lambda i, *_: (i, 0)),
            pl.BlockSpec((TOK_TILE, TOP_K), lambda i, *_: (i, 0)),
            pl.BlockSpec(memory_space=pl.ANY),
        ],
        out_specs=pl.BlockSpec((TOK_TILE, D_MODEL), lambda i, *_: (i, 0)),
        scratch_shapes=[pltpu.VMEM((2, LBUF * SUBLANES, LANES), F32), pltpu.SemaphoreType.DMA((2,))],
    )
    return pl.pallas_call(
        _combine_kernel,
        grid_spec=grid_spec,
        out_shape=jax.ShapeDtypeStruct((N_TOK, D_MODEL), F32),
        compiler_params=_cparams(1),
        name="moe_combine",
    )(*tabs, lp_t, gates_t, ys)


def _expert_kernel(te_ref, tb_ref, nv_ref, nu_ref, x_ref, w1_ref, b1_ref, w2_ref, b2_ref, o_ref, w1b_ref, w2b_ref):
    i = pl.program_id(0)
    used = i < nu_ref[0]
    fresh = jnp.logical_or(i == 0, te_ref[i] != te_ref[jnp.maximum(i - 1, 0)])

    @pl.when(jnp.logical_and(used, fresh))
    def _():
        w1b_ref[...] = w1_ref[0].astype(BF16)
        w2b_ref[...] = w2_ref[0].astype(BF16)

    @pl.when(used)
    def _():
        live = lax.broadcasted_iota(jnp.int32, (TM_X, LANES), 0) < nv_ref[i]
        x = jnp.concatenate(
            [jnp.where(live, x_ref[pl.ds(s, TM_X, stride=SUBLANES), :], 0.0) for s in range(SUBLANES)],
            axis=1).astype(BF16)
        hdn = jnp.dot(x, w1b_ref[...], preferred_element_type=F32) + b1_ref[0]
        glu = jnp.minimum(hdn[:, 0:D_FF], SWIGLU_LIMIT)
        lin = jnp.clip(hdn[:, D_FF:2 * D_FF], -SWIGLU_LIMIT, SWIGLU_LIMIT)
        act = glu * _sigmoid(SWIGLU_ALPHA * glu) * (lin + 1.0)
        y = jnp.dot(act.astype(BF16), w2b_ref[...], preferred_element_type=F32) + b2_ref[0]
        for s in range(SUBLANES):
            o_ref[pl.ds(s, TM_X, stride=SUBLANES), :] = y[:, s * LANES:(s + 1) * LANES]


def _experts(tile_e, tile_blk, tile_rows, n_used, xs, w1, b1, w2, b2):
    rows = pl.BlockSpec((TM_X * SUBLANES, LANES), lambda i, te, tb, nv, nu: (tb[i], 0))
    grid_spec = pltpu.PrefetchScalarGridSpec(
        num_scalar_prefetch=4,
        grid=(N_XTILES,),
        in_specs=[
            rows,
            pl.BlockSpec((1, D_MODEL, 2 * D_FF), lambda i, te, tb, nv, nu: (te[i], 0, 0)),
            pl.BlockSpec((1, 1, 2 * D_FF), lambda i, te, tb, nv, nu: (te[i], 0, 0)),
            pl.BlockSpec((1, D_FF, D_MODEL), lambda i, te, tb, nv, nu: (te[i], 0, 0)),
            pl.BlockSpec((1, 1, D_MODEL), lambda i, te, tb, nv, nu: (te[i], 0, 0)),
        ],
        out_specs=rows,
        scratch_shapes=[pltpu.VMEM((D_MODEL, 2 * D_FF), BF16), pltpu.VMEM((D_FF, D_MODEL), BF16)],
    )
    return pl.pallas_call(
        _expert_kernel,
        grid_spec=grid_spec,
        out_shape=jax.ShapeDtypeStruct((N_EXPERTS * CAP * SUBLANES, LANES), F32),
        compiler_params=_cparams(1),
        name="expert_ffn",
    )(tile_e, tile_blk, tile_rows, n_used, xs, w1, b1.reshape(N_EXPERTS, 1, 2 * D_FF), w2,
      b2.reshape(N_EXPERTS, 1, D_MODEL))


def _expert_schedule(counts):
    tiles_e = jnp.where(counts > 0, (counts + SLACK + TM_X - 1) // TM_X, 0)
    cum = jnp.cumsum(tiles_e)
    first = cum - tiles_e
    n_used = cum[-1]
    i = jnp.arange(N_XTILES, dtype=jnp.int32)
    i = jnp.where(i < n_used, i, jnp.maximum(n_used - 1, 0))
    e = jnp.minimum(jnp.sum((i[:, None] >= cum[None, :]).astype(jnp.int32), axis=1), N_EXPERTS - 1)
    j = i - first[e]
    blk = e * CAP_TILES + j
    live = jnp.clip(counts[e] - j * TM_X, 0, TM_X)
    return (e.astype(jnp.int32), blk.astype(jnp.int32), live.astype(jnp.int32),
            n_used.astype(jnp.int32).reshape(1))


def _moe(h2f, h2b, w_router_t, b_router, w1, b1, w2, b2):
    gates, lp, tcnt, tslot, tdst, counts = _router(h2f, w_router_t, b_router)
    tcnt = tcnt.reshape(-1)
    copies = jnp.sum(((tcnt + SLACK) // DMA_ROWS).reshape(N_TTILES, N_EXPERTS), axis=1).astype(jnp.int32)
    tabs = (tcnt, tslot.reshape(-1), tdst.reshape(-1), copies)
    xs = _dispatch(tabs, h2b, lp)
    ys = _experts(*_expert_schedule(counts[:, 0]), xs, w1, b1, w2, b2)
    return _combine(tabs, lp.T, gates.T, ys)


def _to_blocks(a):
    n, c = a.shape[1], a.shape[2]
    return a.reshape(NSB, SB, n, c).transpose(0, 2, 1, 3)


def _from_blocks(a):
    n, c = a.shape[1], a.shape[3]
    return a.transpose(0, 2, 1, 3).reshape(DEC_BATCH, n, c)


def kernel(x_prompt, x_sample, state_conv_a, state_conv_c, state_pool_d, c_prompt, c_sample,
           w_ada, b_ada, ln_g, ln_b,
           w_in_even, w_dw_a, b_dw_a, ln_a_g, ln_a_b, ln_v_g, ln_v_b, w_spatial, b_spatial, w_out_even,
           w_in_odd, w_conv_c, w_pool_d, scale_pool_d, w_out_odd,
           w_router, b_router, w_exp_in, b_exp_in, w_exp_out, b_exp_out):
    mod = _modulation(jnp.concatenate([c_prompt, c_sample], axis=0), w_ada, b_ada)
    mod_p = mod[:, :, :BATCH].reshape(DEPTH, 6, BATCH, 1, D_MODEL)
    mod_s = mod[:, :, BATCH:]

    xp = x_prompt.reshape(N_PROMPT, D_MODEL)
    xs_ = _to_blocks(x_sample).reshape(N_SAMPLE, D_MODEL)
    pre_p = xp
    pre_s = xs_
    new_a_p, new_a_s, new_v_s, new_c_p, new_c_s, new_d_p, new_d_s = [], [], [], [], [], [], []
    for l in range(DEPTH):
        i = l // 2
        lng, lnb = ln_g[l, 0][None], ln_b[l, 0][None]
        if l % 2 == 0:
            w_in = w_in_even[i].astype(BF16)
            w_out = w_out_even[i].astype(BF16)
            vecs = (w_dw_a[i], b_dw_a[i][None], ln_a_g[i][None], ln_a_b[i][None], ln_v_g[i][None], ln_v_b[i][None])
            mix_p = vecs + (w_spatial[i], b_spatial[i][:, :, None])
            wsp_s = jnp.repeat(w_spatial[i][:, :DEC_SEQ, :DEC_SEQ].transpose(1, 2, 0), CHUNK, axis=-1)
            bsp_s = jnp.repeat(b_spatial[i][:, :DEC_SEQ].T, CHUNK, axis=-1)
            mix_s = vecs + (wsp_s, bsp_s)
            hists = (_to_blocks(state_conv_a[i]),)
        else:
            w_in = w_in_odd[i].astype(BF16)
            w_out = w_out_odd[i].astype(BF16)
            mix_p = mix_s = (w_conv_c[i], w_pool_d[i], scale_pool_d[i][None])
            hists = (_to_blocks(state_conv_c[i]), _to_blocks(state_pool_d[i]))
        outs_p = _prompt_layer(l % 2, pre_p, mod_p[l], w_in, mix_p, w_out, lng, lnb)
        outs_s = _sample_layer(l % 2, pre_s, mod_s[l], w_in, hists, mix_s, w_out, lng, lnb)
        x1_p, h2f_p, h2b_p = outs_p[:3]
        x1_s, h2f_s, h2b_s = outs_s[:3]
        if l % 2 == 0:
            new_a_p.append(outs_p[3])
            new_a_s.append(_from_blocks(outs_s[3]))
            new_v_s.append(_from_blocks(outs_s[4]))
        else:
            new_c_p.append(outs_p[3])
            new_d_p.append(outs_p[4])
            new_c_s.append(_from_blocks(outs_s[3]))
            new_d_s.append(_from_blocks(outs_s[4]))
        f_all = _moe(jnp.concatenate([h2f_p, h2f_s], axis=0), jnp.concatenate([h2b_p, h2b_s], axis=0),
                     w_router[l].T, b_router[l][:, None], w_exp_in[l], b_exp_in[l], w_exp_out[l], b_exp_out[l])
        lpg, lpb = ln_g[l, 1][None], ln_b[l, 1][None]
        pre_p = (x1_p, f_all, mod_p[l], lpg, lpb)
        pre_s = (x1_s, f_all, mod_s[l], lpg, lpb)
    y_p = _final_norm(False, *pre_p).reshape(BATCH, SEQ, D_MODEL)
    y_s = _from_blocks(_final_norm(True, *pre_s).reshape(NSB, DEC_SEQ, SB, D_MODEL))
    return (y_p, y_s, jnp.stack(new_a_p), jnp.stack(new_a_s), jnp.stack(new_v_s),
            jnp.stack(new_c_p), jnp.stack(new_c_s), jnp.stack(new_d_p), jnp.stack(new_d_s))
```
